```python
import functools
import jax, jax.numpy as jnp
from jax import lax
import numpy as np

D_MODEL = 1024
BATCH = 2
SEQ = 8192
DEPTH = 1
DEC_BATCH = 16
DEC_SEQ = 64
PAST_LEN = 4096

CHUNK = 64
Q_BLOCK = 128
H_FOX = 8
HD_FOX = 64
D_FOX = H_FOX * HD_FOX
D_CONV = 512
CONV_WIDTH = 31
CONV_STATE = CONV_WIDTH - 1
H_MEM = 4
HD_MEM = 128
D_MEM = H_MEM * HD_MEM
N_MEM = 256
N_BRANCH = 3
PEER_HEADS = 8
PEER_NKEYS = 128
PEER_N = PEER_NKEYS * PEER_NKEYS
PEER_DK = 256
PEER_TOPK = 16
PEER_TOK_BLOCK = 128
EPS = 1e-6

OFF_Q = 0
OFF_K = OFF_Q + D_FOX
OFF_V = OFF_K + D_FOX
OFF_F = OFF_V + D_FOX
OFF_GLU = OFF_F + H_FOX
OFF_MQ = OFF_GLU + 2 * D_CONV
OFF_G = OFF_MQ + D_MEM
IN_COLS = OFF_G + N_BRANCH * D_MODEL

kernel_name = 'fox_conformer_memory_peer_stream_step'


def rmsnorm(x, w):
    xf = x.astype(jnp.float32)
    xf = xf * lax.rsqrt(jnp.mean(xf * xf, axis=-1, keepdims=True) + EPS)
    return (xf * w.astype(jnp.float32)).astype(x.dtype)


def layernorm(x, w, b):
    xf = x.astype(jnp.float32)
    mu = jnp.mean(xf, axis=-1, keepdims=True)
    xc = xf - mu
    var = jnp.mean(xc * xc, axis=-1, keepdims=True)
    out = xc * lax.rsqrt(var + EPS) * w.astype(jnp.float32) + b.astype(jnp.float32)
    return out.astype(x.dtype)


def project_in(x, norm_w, w_in, b_f):
    N, T = x.shape[:2]
    z = rmsnorm(x, norm_w) @ w_in
    q = z[..., OFF_Q:OFF_K].reshape(N, T, H_FOX, HD_FOX)
    k = z[..., OFF_K:OFF_V].reshape(N, T, H_FOX, HD_FOX)
    v = z[..., OFF_V:OFF_F].reshape(N, T, H_FOX, HD_FOX)
    logf = jax.nn.log_sigmoid((z[..., OFF_F:OFF_GLU] + b_f).astype(jnp.float32))
    glu = z[..., OFF_GLU:OFF_MQ]
    a = glu[..., :D_CONV] * jax.nn.sigmoid(glu[..., D_CONV:])
    mq = z[..., OFF_MQ:OFF_G].reshape(N, T, H_MEM, HD_MEM)
    gates = jax.nn.sigmoid(z[..., OFF_G:].reshape(N, T, N_BRANCH, D_MODEL))
    return q, k, v, logf, a, mq, gates


def fox_block(qb, k, v, q_term, k_term, q_pos, k_pos):
    s = jnp.einsum('nqhd,nkhd->nhqk', qb, k).astype(jnp.float32) * (HD_FOX ** -0.5)
    bias = jnp.swapaxes(q_term, 1, 2)[..., :, None] + jnp.swapaxes(k_term, 1, 2)[..., None, :]
    mask = k_pos[None, :] <= q_pos[:, None]
    s = jnp.where(mask, s + bias, -jnp.inf)
    p = jax.nn.softmax(s, axis=-1).astype(v.dtype)
    return jnp.einsum('nhqk,nkhd->nqhd', p, v)


def fox_prompt(q, k, v, logf):
    N, T = q.shape[:2]
    c = jnp.cumsum(logf, axis=1)
    nb = T // Q_BLOCK
    pos = jnp.arange(T)
    qb = jnp.swapaxes(q.reshape(N, nb, Q_BLOCK, H_FOX, HD_FOX), 0, 1)
    cb = jnp.swapaxes(c.reshape(N, nb, Q_BLOCK, H_FOX), 0, 1)
    pb = pos.reshape(nb, Q_BLOCK)
    out = lax.map(lambda blk: fox_block(blk[0], k, v, blk[1], -c, blk[2], pos), (qb, cb, pb))
    return jnp.swapaxes(out, 0, 1).reshape(N, T, H_FOX, HD_FOX)


def fox_sample(q, k, v, logf, ck, cv, clogf):
    P = ck.shape[1]
    S = q.shape[1]
    clogf = clogf.astype(jnp.float32)
    cnew = jnp.cumsum(logf, axis=1)
    suffix = lax.cumsum(clogf, axis=1, reverse=True) - clogf
    k_term = jnp.concatenate([suffix, -cnew], axis=1)
    keys = jnp.concatenate([ck.astype(k.dtype), k], axis=1)
    vals = jnp.concatenate([cv.astype(v.dtype), v], axis=1)
    return fox_block(q, keys, vals, cnew, k_term, P + jnp.arange(S), jnp.arange(P + S))


def conv_branch(a, hist, w_dw, b_dw, ln_w, ln_b, w_o):
    full = jnp.concatenate([hist.astype(a.dtype), a], axis=1)
    y = lax.conv_general_dilated(full, w_dw[:, None, :], window_strides=(1,), padding='VALID',
                                 dimension_numbers=('NWC', 'WIO', 'NWC'),
                                 feature_group_count=D_CONV) + b_dw
    y = jax.nn.silu(layernorm(y, ln_w, ln_b))
    return y @ w_o, full[:, -CONV_STATE:]


def mem_kv(mem, norm_w, w_kv, k_norm_w):
    N, M = mem.shape[:2]
    m = rmsnorm(mem, norm_w) @ w_kv
    mk = rmsnorm(m[..., :D_MEM].reshape(N, M, H_MEM, HD_MEM), k_norm_w)
    mv = m[..., D_MEM:].reshape(N, M, H_MEM, HD_MEM)
    return mk, mv


def mem_branch(mq, mk, mv, w_o):
    N, T = mq.shape[:2]
    s = jnp.einsum('nqhd,nkhd->nhqk', mq, mk.astype(mq.dtype)).astype(jnp.float32) * (HD_MEM ** -0.5)
    p = jax.nn.softmax(s, axis=-1).astype(mq.dtype)
    o = jnp.einsum('nhqk,nkhd->nqhd', p, mv.astype(mq.dtype))
    return o.reshape(N, T, D_MEM) @ w_o


def peer(h, w_q, keys1, keys2, u, v):
    N, T, D = h.shape
    n = N * T
    nb = -(-n // PEER_TOK_BLOCK)
    flat = jnp.pad(h.reshape(n, D), ((0, nb * PEER_TOK_BLOCK - n), (0, 0)))

    def block(xb):
        B = xb.shape[0]
        q = (xb @ w_q).reshape(B, PEER_HEADS, 2, PEER_DK // 2)
        s1 = jnp.einsum('bhd,hkd->bhk', q[:, :, 0], keys1).astype(jnp.float32)
        s2 = jnp.einsum('bhd,hkd->bhk', q[:, :, 1], keys2).astype(jnp.float32)
        v1, i1 = lax.top_k(s1, PEER_TOPK)
        v2, i2 = lax.top_k(s2, PEER_TOPK)
        cand = (v1[..., :, None] + v2[..., None, :]).reshape(B, PEER_HEADS, PEER_TOPK * PEER_TOPK)
        cidx = (i1[..., :, None] * PEER_NKEYS + i2[..., None, :]).reshape(B, PEER_HEADS, PEER_TOPK * PEER_TOPK)
        top, pos = lax.top_k(cand, PEER_TOPK)
        idx = jnp.take_along_axis(cidx, pos, axis=-1)
        g = jax.nn.softmax(top, axis=-1).astype(xb.dtype)
        ue = u[idx]
        ve = v[idx]
        act = jax.nn.gelu(jnp.einsum('bd,bhkd->bhk', xb, ue), approximate=False)
        return jnp.einsum('bhk,bhkd->bd', g * act, ve)

    out = lax.map(block, flat.reshape(nb, PEER_TOK_BLOCK, D))
    return out.reshape(nb * PEER_TOK_BLOCK, D)[:n].reshape(N, T, D)


def layer(x, conv_hist, mk, mv, fox_fn, w):
    N, T = x.shape[:2]
    q, k, v, logf, a, mq, gates = project_in(x, w['norm1'], w['w_in'], w['b_f'])
    q = rmsnorm(q, w['fox_qn'])
    k = rmsnorm(k, w['fox_kn'])
    o_fox = fox_fn(q, k, v, logf).reshape(N, T, D_FOX) @ w['w_fox_o']
    o_conv, conv_state = conv_branch(a, conv_hist, w['w_dw'], w['b_dw'], w['ln_w'], w['ln_b'], w['w_conv_o'])
    o_mem = mem_branch(rmsnorm(mq, w['mem_qn']), mk, mv, w['w_mem_o'])
    merged = gates[:, :, 0] * o_fox + gates[:, :, 1] * o_conv + gates[:, :, 2] * o_mem
    x = x + merged @ w['w_out']
    x = x + peer(rmsnorm(x, w['norm2']), w['pq'], w['pk1'], w['pk2'], w['pu'], w['pv'])
    return x, k, v, logf, conv_state


def setup_inputs(seed: int = 0) -> dict:
    key = jax.random.key(seed)
    ks = iter(jax.random.split(key, 40))

    def nrm(shape, scale):
        return jax.random.normal(next(ks), shape, jnp.float32) * scale

    def gain(shape):
        return 1.0 + nrm(shape, 0.05)

    L = DEPTH
    return {
        'x_prompt': nrm((BATCH, SEQ, D_MODEL), 1.0),
        'x_sample': nrm((DEC_BATCH, DEC_SEQ, D_MODEL), 1.0),
        'mem_prompt': nrm((BATCH, N_MEM, D_MODEL), 1.0),
        'cache_fox_k': nrm((L, DEC_BATCH, PAST_LEN, H_FOX, HD_FOX), 1.0),
        'cache_fox_v': nrm((L, DEC_BATCH, PAST_LEN, H_FOX, HD_FOX), 1.0),
        'cache_fox_logf': jax.nn.log_sigmoid(2.5 + nrm((L, DEC_BATCH, PAST_LEN, H_FOX), 0.5)),
        'cache_conv': nrm((L, DEC_BATCH, CONV_STATE, D_CONV), 0.5),
        'cache_mem_k': nrm((L, DEC_BATCH, N_MEM, H_MEM, HD_MEM), 1.0),
        'cache_mem_v': nrm((L, DEC_BATCH, N_MEM, H_MEM, HD_MEM), 1.0),
        'norm1_w': gain((L, D_MODEL)),
        'w_in': nrm((L, D_MODEL, IN_COLS), D_MODEL ** -0.5),
        'b_forget': 2.5 + nrm((L, H_FOX), 0.5),
        'fox_q_norm': gain((L, HD_FOX)),
        'fox_k_norm': gain((L, HD_FOX)),
        'w_fox_o': nrm((L, D_FOX, D_MODEL), D_FOX ** -0.5),
        'w_dw': nrm((L, CONV_WIDTH, D_CONV), CONV_WIDTH ** -0.5),
        'b_dw': nrm((L, D_CONV), 0.02),
        'conv_ln_w': gain((L, D_CONV)),
        'conv_ln_b': nrm((L, D_CONV), 0.02),
        'w_conv_o': nrm((L, D_CONV, D_MODEL), D_CONV ** -0.5),
        'mem_norm_w': gain((L, D_MODEL)),
        'w_mem_kv': nrm((L, D_MODEL, 2 * D_MEM), D_MODEL ** -0.5),
        'mem_q_norm': gain((L, HD_MEM)),
        'mem_k_norm': gain((L, HD_MEM)),
        'w_mem_o': nrm((L, D_MEM, D_MODEL), D_MEM ** -0.5),
        'w_out': nrm((L, D_MODEL, D_MODEL), D_MODEL ** -0.5),
        'norm2_w': gain((L, D_MODEL)),
        'peer_w_q': nrm((L, D_MODEL, PEER_HEADS * PEER_DK), D_MODEL ** -0.5),
        'peer_keys1': nrm((L, PEER_HEADS, PEER_NKEYS, PEER_DK // 2), (PEER_DK // 2) ** -0.5),
        'peer_keys2': nrm((L, PEER_HEADS, PEER_NKEYS, PEER_DK // 2), (PEER_DK // 2) ** -0.5),
        'peer_u': nrm((L, PEER_N, D_MODEL), D_MODEL ** -0.5),
        'peer_v': nrm((L, PEER_N, D_MODEL), 0.3),
    }


def reference(x_prompt, x_sample, mem_prompt, cache_fox_k, cache_fox_v, cache_fox_logf, cache_conv,
              cache_mem_k, cache_mem_v, norm1_w, w_in, b_forget, fox_q_norm, fox_k_norm, w_fox_o,
              w_dw, b_dw, conv_ln_w, conv_ln_b, w_conv_o, mem_norm_w, w_mem_kv, mem_q_norm, mem_k_norm,
              w_mem_o, w_out, norm2_w, peer_w_q, peer_keys1, peer_keys2, peer_u, peer_v):
    xp = x_prompt
    xs = x_sample
    pk, pv, pf, pc, pmk, pmv = [], [], [], [], [], []
    sk, sv, sf, sc = [], [], [], []
    for l in range(DEPTH):
        w = dict(norm1=norm1_w[l], w_in=w_in[l], b_f=b_forget[l], fox_qn=fox_q_norm[l], fox_kn=fox_k_norm[l],
                 w_fox_o=w_fox_o[l], w_dw=w_dw[l], b_dw=b_dw[l], ln_w=conv_ln_w[l], ln_b=conv_ln_b[l],
                 w_conv_o=w_conv_o[l], mem_qn=mem_q_norm[l], w_mem_o=w_mem_o[l], w_out=w_out[l],
                 norm2=norm2_w[l], pq=peer_w_q[l], pk1=peer_keys1[l], pk2=peer_keys2[l],
                 pu=peer_u[l], pv=peer_v[l])
        mk_p, mv_p = mem_kv(mem_prompt, mem_norm_w[l], w_mem_kv[l], mem_k_norm[l])
        hist0 = jnp.zeros((xp.shape[0], CONV_STATE, D_CONV), xp.dtype)
        xp, k_p, v_p, f_p, c_p = layer(xp, hist0, mk_p, mv_p, fox_prompt, w)
        pk.append(k_p); pv.append(v_p); pf.append(f_p); pc.append(c_p); pmk.append(mk_p); pmv.append(mv_p)
        fox_s = functools.partial(fox_sample, ck=cache_fox_k[l], cv=cache_fox_v[l], clogf=cache_fox_logf[l])
        xs, k_s, v_s, f_s, c_s = layer(xs, cache_conv[l], cache_mem_k[l], cache_mem_v[l], fox_s, w)
        sk.append(k_s); sv.append(v_s); sf.append(f_s); sc.append(c_s)
    return (xp, xs,
            jnp.stack(pk), jnp.stack(pv), jnp.stack(pf), jnp.stack(pc), jnp.stack(pmk), jnp.stack(pmv),
            jnp.stack(sk), jnp.stack(sv), jnp.stack(sf), jnp.stack(sc))
```

```python
import functools
import math

import jax
import jax.numpy as jnp
from jax import lax
from jax.experimental import pallas as pl
from jax.experimental.pallas import tpu as pltpu

D_MODEL = 1024
H_FOX = 8
HD_FOX = 64
D_FOX = H_FOX * HD_FOX
D_CONV = 512
CONV_WIDTH = 31
CONV_STATE = CONV_WIDTH - 1
H_MEM = 4
HD_MEM = 128
D_MEM = H_MEM * HD_MEM
N_MEM = 256
N_BRANCH = 3
PEER_HEADS = 8
PEER_NKEYS = 128
PEER_N = PEER_NKEYS * PEER_NKEYS
PEER_DK = 256
PEER_TOPK = 16
EPS = 1e-6

OFF_Q = 0
OFF_K = OFF_Q + D_FOX
OFF_V = OFF_K + D_FOX
OFF_F = OFF_V + D_FOX
OFF_GLU = OFF_F + H_FOX
OFF_MQ = OFF_GLU + 2 * D_CONV
OFF_G = OFF_MQ + D_MEM

LANES = 128
CONV_HALO = 32
VMEM_LIMIT = 56 * 1024 * 1024

F32 = jnp.float32
BF16 = jnp.bfloat16
NEG_INF = float("-inf")


def _dot(a, b):
    return jnp.dot(a.astype(BF16), b.astype(BF16), preferred_element_type=F32)


def _dot_nt(a, b):
    return lax.dot_general(a.astype(BF16), b.astype(BF16), (((1,), (1,)), ((), ())),
                           preferred_element_type=F32)


def _split3(x):
    hi = x.astype(BF16)
    r = x - hi.astype(F32)
    mid = r.astype(BF16)
    lo = (r - mid.astype(F32)).astype(BF16)
    return hi, mid, lo


def _dot_exact_rhs01(x, m01):
    hi, mid, lo = _split3(x)
    return (jnp.dot(hi, m01, preferred_element_type=F32) + jnp.dot(mid, m01, preferred_element_type=F32)
            + jnp.dot(lo, m01, preferred_element_type=F32))


def _rms_rows(x, w):
    return x * lax.rsqrt(jnp.mean(x * x, axis=-1, keepdims=True) + EPS) * w


def _params(sem):
    return pltpu.CompilerParams(dimension_semantics=sem, vmem_limit_bytes=VMEM_LIMIT)


def _full(shape):
    nd = len(shape)
    return pl.BlockSpec(shape, lambda *_: (0,) * nd)


def _proj_kernel(x_ref, n1_ref, wqkv_ref, wf_ref, bf_ref, wglu_ref, wmq_ref, qn_ref, kn_ref, mqn_ref, bd_ref,
                 q_out, kf_out, kb_out, vf_out, vb_out, logf_out, logft_out, a_out, mq_out):
    x = x_ref[...]
    h = _rms_rows(x, n1_ref[...]).astype(BF16)
    z = jnp.dot(h, wqkv_ref[...], preferred_element_type=F32)
    bd = bd_ref[...]

    def head_norm(t, w):
        sq = t * t
        hi = sq.astype(BF16)
        lo = (sq - hi.astype(F32)).astype(BF16)
        ss = jnp.dot(hi, bd, preferred_element_type=F32) + jnp.dot(lo, bd, preferred_element_type=F32)
        return t * lax.rsqrt(ss * (1.0 / HD_FOX) + EPS) * w

    q = head_norm(z[:, OFF_Q:OFF_K], qn_ref[...])
    k = head_norm(z[:, OFF_K:OFF_V], kn_ref[...])
    v = z[:, OFF_V:OFF_F]
    q_out[...] = (q * (HD_FOX ** -0.5)).astype(BF16)
    kf_out[...] = k
    kb_out[...] = k.astype(BF16)
    vf_out[...] = v
    vb_out[...] = v.astype(BF16)

    zf = jnp.dot(h, wf_ref[...], preferred_element_type=F32) + bf_ref[...]
    lf = jnp.minimum(zf, 0.0) - jnp.log1p(jnp.exp(-jnp.abs(zf)))
    logf_out[...] = lf[:, :H_FOX]
    logft_out[...] = lf.T[:H_FOX, :]

    g = jnp.dot(h, wglu_ref[...], preferred_element_type=F32)
    a_out[...] = g[:, :D_CONV] * jax.nn.sigmoid(g[:, D_CONV:])

    mq = jnp.dot(h, wmq_ref[...], preferred_element_type=F32)
    mqn = mqn_ref[...]
    parts = []
    for hh in range(H_MEM):
        blk = mq[:, hh * HD_MEM:(hh + 1) * HD_MEM]
        parts.append(_rms_rows(blk, mqn) * (HD_MEM ** -0.5))
    mq_out[...] = jnp.concatenate(parts, axis=-1).astype(BF16)


def _project(x, w, tb):
    nt = x.shape[0]
    grid = (nt // tb,)
    row = lambda c: pl.BlockSpec((tb, c), lambda i: (i, 0))
    out_shape = (
        jax.ShapeDtypeStruct((nt, D_FOX), BF16),
        jax.ShapeDtypeStruct((nt, D_FOX), F32),
        jax.ShapeDtypeStruct((nt, D_FOX), BF16),
        jax.ShapeDtypeStruct((nt, D_FOX), F32),
        jax.ShapeDtypeStruct((nt, D_FOX), BF16),
        jax.ShapeDtypeStruct((nt, H_FOX), F32),
        jax.ShapeDtypeStruct((H_FOX, nt), F32),
        jax.ShapeDtypeStruct((nt, D_CONV), F32),
        jax.ShapeDtypeStruct((nt, D_MEM), BF16),
    )
    out_specs = (row(D_FOX), row(D_FOX), row(D_FOX), row(D_FOX), row(D_FOX), row(H_FOX),
                 pl.BlockSpec((H_FOX, tb), lambda i: (0, i)), row(D_CONV), row(D_MEM))
    ins = (x, w['norm1'], w['w_qkv'], w['w_f'], w['b_f'], w['w_glu'], w['w_mq'], w['fox_qn'], w['fox_kn'],
           w['mem_qn'], w['bd64'])
    in_specs = [row(D_MODEL)] + [_full(a.shape) for a in ins[1:]]
    return pl.pallas_call(_proj_kernel, grid=grid, in_specs=in_specs, out_specs=out_specs, out_shape=out_shape,
                          compiler_params=_params(("parallel",)), name="proj_in")(*ins)


def _cumsum_kernel(lt_ref, tri_ref, c_ref, carry_ref, *, blocks_per_seq):
    i = pl.program_id(0)

    @pl.when(i % blocks_per_seq == 0)
    def _():
        carry_ref[...] = jnp.zeros_like(carry_ref)

    c = _dot_exact_rhs01(lt_ref[...], tri_ref[...]) + carry_ref[:, 0:1]
    c_ref[0] = c
    carry_ref[...] = jnp.broadcast_to(c[:, -1:], carry_ref.shape)


def _cumsum_seq(logft, seq_len, tk, tri):
    nt = logft.shape[1]
    return pl.pallas_call(
        functools.partial(_cumsum_kernel, blocks_per_seq=seq_len // tk),
        grid=(nt // tk,),
        in_specs=[pl.BlockSpec((H_FOX, tk), lambda i: (0, i)), _full(tri.shape)],
        out_specs=pl.BlockSpec((1, H_FOX, tk), lambda i: (i, 0, 0)),
        out_shape=jax.ShapeDtypeStruct((nt // tk, H_FOX, tk), F32),
        scratch_shapes=[pltpu.VMEM((H_FOX, LANES), F32)],
        compiler_params=_params(("arbitrary",)), name="logf_cumsum")(logft, tri)


def _fox_prompt_kernel(q_ref, k_ref, v_ref, c_ref, o_ref, *, tq):
    hp = pl.program_id(1)
    qi = pl.program_id(2)
    lane = lax.broadcasted_iota(jnp.int32, (tq, LANES), 1)
    q2 = q_ref[...]
    zero = jnp.zeros_like(q2)
    qs = (jnp.where(lane < HD_FOX, q2, zero), jnp.where(lane >= HD_FOX, q2, zero))

    def attend(ki, carry, masked):
        k0 = pl.multiple_of(ki * tq, tq)
        k2 = k_ref[pl.ds(k0, tq), :]
        v2 = v_ref[pl.ds(k0, tq), :]
        out = []
        for j in range(2):
            m, l, acc = carry[j]
            s = _dot_nt(qs[j], k2) - c_ref[ki, pl.ds(2 * hp + j, 1), :]
            if masked:
                r = lax.broadcasted_iota(jnp.int32, (tq, tq), 0)
                c = lax.broadcasted_iota(jnp.int32, (tq, tq), 1)
                s = jnp.where(c <= r, s, NEG_INF)
            m_new = jnp.maximum(m, jnp.max(s, axis=-1, keepdims=True))
            alpha = jnp.exp(m - m_new)
            p = jnp.exp(s - m_new)
            l = alpha * l + jnp.sum(p, axis=-1, keepdims=True)
            acc = alpha * acc + jnp.dot(p.astype(BF16), v2, preferred_element_type=F32)
            out.append((m_new, l, acc))
        return tuple(out)

    init = tuple((jnp.full((tq, 1), NEG_INF, F32), jnp.zeros((tq, 1), F32), jnp.zeros((tq, LANES), F32))
                 for _ in range(2))
    carry = lax.fori_loop(0, qi, lambda ki, c: attend(ki, c, False), init)
    (_, l0, a0), (_, l1, a1) = attend(qi, carry, True)
    o_ref[...] = jnp.where(lane < HD_FOX, a0 / l0, a1 / l1).astype(o_ref.dtype)


def _fox_prompt(qb, kb, vb, cum, n_seq, seq_len, tq):
    nq = seq_len // tq
    q3 = qb.reshape(n_seq, seq_len, D_FOX)
    k3 = kb.reshape(n_seq, seq_len, D_FOX)
    v3 = vb.reshape(n_seq, seq_len, D_FOX)
    out = pl.pallas_call(
        functools.partial(_fox_prompt_kernel, tq=tq),
        grid=(n_seq, H_FOX // 2, nq),
        in_specs=[pl.BlockSpec((None, tq, LANES), lambda n, h, i: (n, i, h)),
                  pl.BlockSpec((None, seq_len, LANES), lambda n, h, i: (n, 0, h)),
                  pl.BlockSpec((None, seq_len, LANES), lambda n, h, i: (n, 0, h)),
                  pl.BlockSpec((nq, H_FOX, tq), lambda n, h, i: (n, 0, 0))],
        out_specs=pl.BlockSpec((None, tq, LANES), lambda n, h, i: (n, i, h)),
        out_shape=jax.ShapeDtypeStruct((n_seq, seq_len, D_FOX), BF16),
        compiler_params=_params(("parallel", "parallel", "arbitrary")), name="fox_prompt")(q3, k3, v3, cum)
    return out.reshape(n_seq * seq_len, D_FOX)


def _fox_sample_kernel(q_ref, kn_ref, vn_ref, ln_ref, ck_ref, cv_ref, cl_ref, tri_ref, o_ref,
                       qbd_ref, m_ref, l_ref, acc_ref, carry_ref, *, n_cache_blocks, s_new):
    ki = pl.program_id(1)
    rows = H_FOX * s_new

    @pl.when(ki == 0)
    def _():
        q = q_ref[...]
        lane = lax.broadcasted_iota(jnp.int32, q.shape, 1)
        zero = jnp.zeros_like(q)
        for h in range(H_FOX):
            sel = (lane >= h * HD_FOX) & (lane < (h + 1) * HD_FOX)
            qbd_ref[h * s_new:(h + 1) * s_new, :] = jnp.where(sel, q, zero)
        m_ref[...] = jnp.full_like(m_ref, NEG_INF)
        l_ref[...] = jnp.zeros_like(l_ref)
        acc_ref[...] = jnp.zeros_like(acc_ref)
        carry_ref[...] = jnp.zeros_like(carry_ref)

    def update(s, vblk):
        m = m_ref[...]
        m_new = jnp.maximum(m, jnp.max(s, axis=-1, keepdims=True))
        alpha = jnp.exp(m - m_new)
        p = jnp.exp(s - m_new)
        l_ref[...] = alpha * l_ref[...] + jnp.sum(p, axis=-1, keepdims=True)
        acc_ref[...] = alpha * acc_ref[...] + jnp.dot(p.astype(BF16), vblk, preferred_element_type=F32)
        m_ref[...] = m_new

    def biased(s, cum):
        return jnp.concatenate([s[h * s_new:(h + 1) * s_new, :] - cum[h:h + 1, :] for h in range(H_FOX)], axis=0)

    @pl.when(ki < n_cache_blocks)
    def _():
        tk = ck_ref.shape[0]
        cum = _dot_exact_rhs01(cl_ref[...], tri_ref[...]) + carry_ref[:, 0:1]
        carry_ref[...] = jnp.broadcast_to(cum[:, -1:], carry_ref.shape)
        s = _dot_nt(qbd_ref[...], ck_ref[...])
        update(biased(s, cum), cv_ref[...].astype(BF16))
        del tk

    @pl.when(ki == n_cache_blocks)
    def _():
        tri = tri_ref[0:s_new, 0:s_new]
        cum = _dot_exact_rhs01(ln_ref[...], tri) + carry_ref[:, 0:1]
        s = biased(_dot_nt(qbd_ref[...], kn_ref[...]), cum)
        t = lax.broadcasted_iota(jnp.int32, (rows, s_new), 0) % s_new
        c = lax.broadcasted_iota(jnp.int32, (rows, s_new), 1)
        update(jnp.where(c <= t, s, NEG_INF), vn_ref[...])
        o = acc_ref[...] / l_ref[...]
        lane = lax.broadcasted_iota(jnp.int32, (s_new, D_FOX), 1)
        res = jnp.zeros((s_new, D_FOX), F32)
        for h in range(H_FOX):
            sel = (lane >= h * HD_FOX) & (lane < (h + 1) * HD_FOX)
            res = jnp.where(sel, o[h * s_new:(h + 1) * s_new, :], res)
        o_ref[...] = res.astype(o_ref.dtype)


def _fox_sample(qb, kb, vb, logft, ck, cv, clogf, tri, tk):
    n_seq, past = ck.shape[0], ck.shape[1]
    s_new = qb.shape[0] // n_seq
    ncb = past // tk
    q3 = qb.reshape(n_seq, s_new, D_FOX)
    k3 = kb.reshape(n_seq, s_new, D_FOX)
    v3 = vb.reshape(n_seq, s_new, D_FOX)
    ln = jnp.swapaxes(logft.reshape(H_FOX, n_seq, s_new), 0, 1)
    cl = jnp.swapaxes(clogf.astype(F32), 1, 2)
    rows = H_FOX * s_new
    new = pl.BlockSpec((None, s_new, D_FOX), lambda n, k: (n, 0, 0))
    cache = pl.BlockSpec((None, tk, D_FOX), lambda n, k: (n, jnp.minimum(k, ncb - 1), 0))
    out = pl.pallas_call(
        functools.partial(_fox_sample_kernel, n_cache_blocks=ncb, s_new=s_new),
        grid=(n_seq, ncb + 1),
        in_specs=[new, new, new,
                  pl.BlockSpec((None, H_FOX, s_new), lambda n, k: (n, 0, 0)),
                  cache, cache,
                  pl.BlockSpec((None, H_FOX, tk), lambda n, k: (n, 0, jnp.minimum(k, ncb - 1))),
                  _full(tri.shape)],
        out_specs=new,
        out_shape=jax.ShapeDtypeStruct((n_seq, s_new, D_FOX), BF16),
        scratch_shapes=[pltpu.VMEM((rows, D_FOX), BF16), pltpu.VMEM((rows, 1), F32), pltpu.VMEM((rows, 1), F32),
                        pltpu.VMEM((rows, D_FOX), F32), pltpu.VMEM((H_FOX, LANES), F32)],
        compiler_params=_params(("parallel", "arbitrary")), name="fox_sample")(q3, k3, v3, ln, ck, cv, cl, tri)
    return out.reshape(n_seq * s_new, D_FOX)


def _conv_kernel(a_ref, tail_ref, hist_ref, wdw_ref, bdw_ref, lnw_ref, lnb_ref, y_ref, win_ref, *, tb):
    i = pl.program_id(1)

    @pl.when(i == 0)
    def _():
        win_ref[0:CONV_HALO, :] = hist_ref[...]

    @pl.when(i > 0)
    def _():
        win_ref[0:CONV_HALO, :] = tail_ref[...]

    win_ref[CONV_HALO:CONV_HALO + tb, :] = a_ref[...]
    wdw = wdw_ref[...]
    first = CONV_HALO - CONV_STATE
    y = jnp.zeros((tb, D_CONV), F32) + bdw_ref[...]
    for w in range(CONV_WIDTH):
        y = y + win_ref[first + w:first + w + tb, :] * wdw[w:w + 1, :]
    mu = jnp.mean(y, axis=-1, keepdims=True)
    yc = y - mu
    var = jnp.mean(yc * yc, axis=-1, keepdims=True)
    yn = yc * lax.rsqrt(var + EPS) * lnw_ref[...] + lnb_ref[...]
    y_ref[...] = (yn * jax.nn.sigmoid(yn)).astype(y_ref.dtype)


def _conv_branch(a, hist, w, n_seq, seq_len, tb):
    a3 = a.reshape(n_seq, seq_len, D_CONV)
    hist32 = jnp.pad(hist.astype(F32), ((0, 0), (CONV_HALO - CONV_STATE, 0), (0, 0)))
    r = tb // CONV_HALO
    out = pl.pallas_call(
        functools.partial(_conv_kernel, tb=tb),
        grid=(n_seq, seq_len // tb),
        in_specs=[pl.BlockSpec((None, tb, D_CONV), lambda n, i: (n, i, 0)),
                  pl.BlockSpec((None, CONV_HALO, D_CONV), lambda n, i: (n, jnp.maximum(i * r - 1, 0), 0)),
                  pl.BlockSpec((None, CONV_HALO, D_CONV), lambda n, i: (n, 0, 0)),
                  _full(w['w_dw'].shape), _full(w['b_dw'].shape), _full(w['ln_w'].shape), _full(w['ln_b'].shape)],
        out_specs=pl.BlockSpec((None, tb, D_CONV), lambda n, i: (n, i, 0)),
        out_shape=jax.ShapeDtypeStruct((n_seq, seq_len, D_CONV), BF16),
        scratch_shapes=[pltpu.VMEM((CONV_HALO + tb, D_CONV), F32)],
        compiler_params=_params(("parallel", "arbitrary")), name="conv_branch")(
            a3, a3, hist32, w['w_dw'], w['b_dw'], w['ln_w'], w['ln_b'])
    return out.reshape(n_seq * seq_len, D_CONV)


def _memkv_kernel(mem_ref, nw_ref, wkv_ref, kn_ref, mk_ref, mv_ref):
    h = _rms_rows(mem_ref[...], nw_ref[...]).astype(BF16)
    m = jnp.dot(h, wkv_ref[...], preferred_element_type=F32)
    kn = kn_ref[...]
    parts = [_rms_rows(m[:, hh * HD_MEM:(hh + 1) * HD_MEM], kn) for hh in range(H_MEM)]
    mk_ref[...] = jnp.concatenate(parts, axis=-1)
    mv_ref[...] = m[:, D_MEM:]


def _mem_kv(mem, w):
    nt = mem.shape[0]
    tb = N_MEM
    row = lambda c: pl.BlockSpec((tb, c), lambda i: (i, 0))
    return pl.pallas_call(
        _memkv_kernel, grid=(nt // tb,),
        in_specs=[row(D_MODEL), _full(w['mem_norm'].shape), _full(w['w_mem_kv'].shape), _full(w['mem_kn'].shape)],
        out_specs=(row(D_MEM), row(D_MEM)),
        out_shape=(jax.ShapeDtypeStruct((nt, D_MEM), F32), jax.ShapeDtypeStruct((nt, D_MEM), F32)),
        compiler_params=_params(("parallel",)), name="mem_kv")(mem, w['mem_norm'], w['w_mem_kv'], w['mem_kn'])


def _merge_kernel(x_ref, of_ref, yc_ref, mq_ref, mk_ref, mv_ref, n1_ref, wg_ref, wfo_ref, wco_ref, wmo_ref,
                  wout_ref, x1_ref):
    x = x_ref[...]
    h = _rms_rows(x, n1_ref[...]).astype(BF16)
    mq = mq_ref[...]
    mk = mk_ref[...].astype(BF16)
    mv = mv_ref[...].astype(BF16)
    heads = []
    for hh in range(H_MEM):
        sl = slice(hh * HD_MEM, (hh + 1) * HD_MEM)
        s = _dot_nt(mq[:, sl], mk[:, sl])
        p = jnp.exp(s - jnp.max(s, axis=-1, keepdims=True))
        p = p / jnp.sum(p, axis=-1, keepdims=True)
        heads.append(jnp.dot(p.astype(BF16), mv[:, sl], preferred_element_type=F32))
    o_mem = jnp.concatenate(heads, axis=-1).astype(BF16)

    def gate(b):
        zg = jnp.dot(h, wg_ref[:, b * D_MODEL:(b + 1) * D_MODEL], preferred_element_type=F32)
        return jax.nn.sigmoid(zg)

    merged = gate(0) * jnp.dot(of_ref[...], wfo_ref[...], preferred_element_type=F32)
    merged = merged + gate(1) * jnp.dot(yc_ref[...], wco_ref[...], preferred_element_type=F32)
    merged = merged + gate(2) * jnp.dot(o_mem, wmo_ref[...], preferred_element_type=F32)
    x1_ref[...] = x + jnp.dot(merged.astype(BF16), wout_ref[...], preferred_element_type=F32)


def _merge(x, o_fox, y_conv, mqn, mk, mv, w, seq_len, tb):
    nt = x.shape[0]
    bps = seq_len // tb
    row = lambda c: pl.BlockSpec((tb, c), lambda i: (i, 0))
    mem = pl.BlockSpec((N_MEM, D_MEM), lambda i: (i // bps, 0))
    ws = (w['norm1'], w['w_g'], w['w_fox_o'], w['w_conv_o'], w['w_mem_o'], w['w_out'])
    return pl.pallas_call(
        _merge_kernel, grid=(nt // tb,),
        in_specs=[row(D_MODEL), row(D_FOX), row(D_CONV), row(D_MEM), mem, mem] + [_full(a.shape) for a in ws],
        out_specs=row(D_MODEL),
        out_shape=jax.ShapeDtypeStruct((nt, D_MODEL), F32),
        compiler_params=_params(("parallel",)), name="merge")(x, o_fox, y_conv, mqn, mk, mv, *ws)


TOPN = PEER_TOPK + 1
TOP_ROWS = 24


def _top_sorted(vals, n, out_ref):
    out_ref[...] = jnp.full(out_ref.shape, NEG_INF, F32)

    def body(r, v):
        m = jnp.max(v, axis=0, keepdims=True)
        out_ref[pl.ds(r, 1), :] = m
        return jnp.where(v == m, NEG_INF, v)

    lax.fori_loop(0, n, body, vals)


def _peer_route_kernel(x_ref, n2_ref, pqt_ref, k1_ref, k2_ref, xn_ref, a1_ref, e2_ref, t1_ref, s2_ref,
                       qt_ref, v1_ref, v2_ref, tt_ref):
    xn = _rms_rows(x_ref[...], n2_ref[...]).astype(BF16)
    xn_ref[...] = xn
    qt_ref[...] = _dot_nt(pqt_ref[...], xn).astype(BF16)
    half = PEER_DK // 2
    nk8 = PEER_TOPK // 2

    def per_head(h, _):
        q1 = qt_ref[pl.ds(pl.multiple_of(h * PEER_DK, PEER_DK), half), :]
        q2 = qt_ref[pl.ds(pl.multiple_of(h * PEER_DK + half, half), half), :]
        s1 = jnp.dot(k1_ref[h], q1, preferred_element_type=F32)
        s2 = jnp.dot(k2_ref[h], q2, preferred_element_type=F32)
        _top_sorted(s1, TOPN, v1_ref)
        _top_sorted(s2, TOPN, v2_ref)
        v1 = v1_ref[...]
        v2 = v2_ref[...]
        pieces = [v1 + v2[0:1, :]]
        pieces += [v1[0:nk8, :] + v2[b:b + 1, :] for b in range(1, nk8)]
        pieces += [v1[0:1, :] + v2[nk8:TOP_ROWS, :]]
        _top_sorted(jnp.concatenate(pieces, axis=0), TOPN, tt_ref)
        tt = tt_ref[...]
        top = tt[0:PEER_TOPK, :]
        tau = 0.5 * (tt[PEER_TOPK - 1:PEER_TOPK, :] + tt[PEER_TOPK:TOPN, :])
        z = jnp.sum(jnp.exp(top - tt[0:1, :]), axis=0, keepdims=True)
        a1_ref[h] = jnp.exp(s1 - v1[0:1, :]) / z
        e2_ref[h] = jnp.exp(s2 - v2[0:1, :])
        t1_ref[h] = tau - s1
        s2_ref[h] = s2
        return 0

    lax.fori_loop(0, PEER_HEADS, per_head, 0)


def _peer_route(x1, w, tb):
    nt = x1.shape[0]
    sel = pl.BlockSpec((PEER_HEADS, PEER_NKEYS, tb), lambda i: (0, 0, i))
    sel_shape = jax.ShapeDtypeStruct((PEER_HEADS, PEER_NKEYS, nt), F32)
    return pl.pallas_call(
        _peer_route_kernel, grid=(nt // tb,),
        in_specs=[pl.BlockSpec((tb, D_MODEL), lambda i: (i, 0)), _full(w['norm2'].shape), _full(w['pq_t'].shape),
                  _full(w['pk1'].shape), _full(w['pk2'].shape)],
        out_specs=(pl.BlockSpec((tb, D_MODEL), lambda i: (i, 0)), sel, sel, sel, sel),
        out_shape=(jax.ShapeDtypeStruct((nt, D_MODEL), BF16), sel_shape, sel_shape, sel_shape, sel_shape),
        scratch_shapes=[pltpu.VMEM((PEER_HEADS * PEER_DK, tb), BF16), pltpu.VMEM((TOP_ROWS, tb), F32),
                        pltpu.VMEM((TOP_ROWS, tb), F32), pltpu.VMEM((TOP_ROWS, tb), F32)],
        compiler_params=_params(("parallel",)), name="peer_route")(x1, w['norm2'], w['pq_t'], w['pk1'], w['pk2'])


def _peer_dense_kernel(x1_ref, xn_ref, a1_ref, e2_ref, t1_ref, s2_ref, u_ref, vt_ref, y_ref, w_ref, acc_ref,
                       *, groups):
    e = pl.program_id(1)

    @pl.when(e == 0)
    def _():
        acc_ref[...] = jnp.zeros_like(acc_ref)

    xn = xn_ref[...]
    for g in range(groups):
        i1 = e * groups + g
        st = _dot_nt(u_ref[g * PEER_NKEYS:(g + 1) * PEER_NKEYS, :], xn)
        gate = jnp.zeros_like(st)
        for h in range(PEER_HEADS):
            keep = s2_ref[h] >= t1_ref[h, pl.ds(i1, 1), :]
            gate = gate + jnp.where(keep, e2_ref[h], 0.0) * a1_ref[h, pl.ds(i1, 1), :]
        act = 0.5 * st * (1.0 + lax.erf(st * (1.0 / math.sqrt(2.0))))
        w_ref[g * PEER_NKEYS:(g + 1) * PEER_NKEYS, :] = (gate * act).astype(BF16)
    acc_ref[...] += jnp.dot(vt_ref[...], w_ref[...], preferred_element_type=F32)

    @pl.when(e == pl.num_programs(1) - 1)
    def _():
        y_ref[...] = x1_ref[...] + acc_ref[...].T


def _peer_dense(x1, xn, a1, e2, t1, s2, w, tb, eb):
    nt = x1.shape[0]
    groups = eb // PEER_NKEYS
    sel = pl.BlockSpec((PEER_HEADS, PEER_NKEYS, tb), lambda i, e: (0, 0, i))
    return pl.pallas_call(
        functools.partial(_peer_dense_kernel, groups=groups),
        grid=(nt // tb, PEER_N // eb),
        in_specs=[pl.BlockSpec((tb, D_MODEL), lambda i, e: (i, 0)), pl.BlockSpec((tb, D_MODEL), lambda i, e: (i, 0)),
                  sel, sel, sel, sel,
                  pl.BlockSpec((eb, D_MODEL), lambda i, e: (e, 0)),
                  pl.BlockSpec((D_MODEL, eb), lambda i, e: (0, e))],
        out_specs=pl.BlockSpec((tb, D_MODEL), lambda i, e: (i, 0)),
        out_shape=jax.ShapeDtypeStruct((nt, D_MODEL), F32),
        scratch_shapes=[pltpu.VMEM((eb, tb), BF16), pltpu.VMEM((D_MODEL, tb), F32)],
        compiler_params=_params(("parallel", "arbitrary")), name="peer_dense")(
            x1, xn, a1, e2, t1, s2, w['pu'], w['pv_t'])


def _block_sizes(n_seq, seq_len):
    nt = n_seq * seq_len
    return dict(
        proj=min(512, nt),
        att=min(512, seq_len),
        conv=min(512, seq_len),
        merge=min(256, seq_len),
        route=min(256, nt),
        dense=min(256, nt),
        experts=1024,
    )


def _layer(x, hist, mk, mv, w, n_seq, seq_len, cache):
    bs = _block_sizes(n_seq, seq_len)
    qb, kf, kb, vf, vb, logf, logft, a, mqn = _project(x, w, bs['proj'])
    if cache is None:
        cum = _cumsum_seq(logft, seq_len, bs['att'], w['tri'][:bs['att'], :bs['att']])
        o_fox = _fox_prompt(qb, kb, vb, cum, n_seq, seq_len, bs['att'])
    else:
        ck, cv, clogf = cache
        o_fox = _fox_sample(qb, kb, vb, logft, ck, cv, clogf, w['tri'], w['tri'].shape[0])
    y_conv = _conv_branch(a, hist, w, n_seq, seq_len, bs['conv'])
    x1 = _merge(x, o_fox, y_conv, mqn, mk, mv, w, seq_len, bs['merge'])
    xn, a1, e2, t1, s2 = _peer_route(x1, w, bs['route'])
    y = _peer_dense(x1, xn, a1, e2, t1, s2, w, bs['dense'], bs['experts'])
    conv_state = a.reshape(n_seq, seq_len, D_CONV)[:, seq_len - CONV_STATE:, :]
    return y, kf, vf, logf, conv_state


def kernel(x_prompt, x_sample, mem_prompt, cache_fox_k, cache_fox_v, cache_fox_logf, cache_conv, cache_mem_k, cache_mem_v, norm1_w, w_in, b_forget, fox_q_norm, fox_k_norm, w_fox_o, w_dw, b_dw, conv_ln_w, conv_ln_b, w_conv_o, mem_norm_w, w_mem_kv, mem_q_norm, mem_k_norm, w_mem_o, w_out, norm2_w, peer_w_q, peer_keys1, peer_keys2, peer_u, peer_v):
    depth = w_in.shape[0]
    batch, seq = x_prompt.shape[:2]
    dbatch, dseq = x_sample.shape[:2]
    past = cache_fox_k.shape[2]
    tri_n = 1024
    tri = (lax.broadcasted_iota(jnp.int32, (tri_n, tri_n), 0) <= lax.broadcasted_iota(jnp.int32, (tri_n, tri_n), 1)
           ).astype(BF16)
    gid = lax.broadcasted_iota(jnp.int32, (D_FOX, D_FOX), 0) // HD_FOX
    bd64 = (gid == gid.T).astype(BF16)

    xp = x_prompt.reshape(batch * seq, D_MODEL)
    xs = x_sample.reshape(dbatch * dseq, D_MODEL)
    outs = [[] for _ in range(10)]
    for l in range(depth):
        wi = w_in[l]
        row2 = lambda v: v.reshape(1, -1).astype(F32)
        w = dict(
            norm1=row2(norm1_w[l]),
            w_qkv=wi[:, OFF_Q:OFF_F].astype(BF16),
            w_f=jnp.pad(wi[:, OFF_F:OFF_GLU], ((0, 0), (0, LANES - H_FOX))).astype(BF16),
            b_f=jnp.pad(b_forget[l].astype(F32), (0, LANES - H_FOX)).reshape(1, LANES),
            w_glu=wi[:, OFF_GLU:OFF_MQ].astype(BF16),
            w_mq=wi[:, OFF_MQ:OFF_G].astype(BF16),
            w_g=wi[:, OFF_G:].astype(BF16),
            fox_qn=row2(jnp.tile(fox_q_norm[l], H_FOX)),
            fox_kn=row2(jnp.tile(fox_k_norm[l], H_FOX)),
            mem_qn=row2(mem_q_norm[l]),
            mem_kn=row2(mem_k_norm[l]),
            mem_norm=row2(mem_norm_w[l]),
            w_mem_kv=w_mem_kv[l].astype(BF16),
            bd64=bd64, tri=tri,
            w_dw=w_dw[l].astype(F32), b_dw=row2(b_dw[l]), ln_w=row2(conv_ln_w[l]), ln_b=row2(conv_ln_b[l]),
            w_fox_o=w_fox_o[l].astype(BF16), w_conv_o=w_conv_o[l].astype(BF16), w_mem_o=w_mem_o[l].astype(BF16),
            w_out=w_out[l].astype(BF16),
            norm2=row2(norm2_w[l]),
            pq_t=peer_w_q[l].T.astype(BF16),
            pk1=peer_keys1[l].astype(BF16), pk2=peer_keys2[l].astype(BF16),
            pu=peer_u[l].astype(BF16), pv_t=peer_v[l].T.astype(BF16),
        )
        mk_p, mv_p = _mem_kv(mem_prompt.reshape(batch * N_MEM, D_MODEL), w)
        hist0 = jnp.zeros((batch, CONV_STATE, D_CONV), F32)
        xp, k_p, v_p, f_p, c_p = _layer(xp, hist0, mk_p, mv_p, w, batch, seq, None)
        cache = (cache_fox_k[l].reshape(dbatch, past, D_FOX), cache_fox_v[l].reshape(dbatch, past, D_FOX),
                 cache_fox_logf[l])
        mk_s = cache_mem_k[l].reshape(dbatch * N_MEM, D_MEM).astype(F32)
        mv_s = cache_mem_v[l].reshape(dbatch * N_MEM, D_MEM).astype(F32)
        xs, k_s, v_s, f_s, c_s = _layer(xs, cache_conv[l], mk_s, mv_s, w, dbatch, dseq, cache)
        vals = (k_p.reshape(batch, seq, H_FOX, HD_FOX), v_p.reshape(batch, seq, H_FOX, HD_FOX),
                f_p.reshape(batch, seq, H_FOX), c_p,
                mk_p.reshape(batch, N_MEM, H_MEM, HD_MEM), mv_p.reshape(batch, N_MEM, H_MEM, HD_MEM),
                k_s.reshape(dbatch, dseq, H_FOX, HD_FOX), v_s.reshape(dbatch, dseq, H_FOX, HD_FOX),
                f_s.reshape(dbatch, dseq, H_FOX), c_s)
        for o, v in zip(outs, vals):
            o.append(v)
    return (xp.reshape(batch, seq, D_MODEL), xs.reshape(dbatch, dseq, D_MODEL)) + tuple(jnp.stack(o) for o in outs)
```

```python
import functools
import math

import jax
import jax.numpy as jnp
from jax import lax
from jax.experimental import pallas as pl
from jax.experimental.pallas import tpu as pltpu

D_MODEL = 1024
H_FOX = 8
HD_FOX = 64
D_FOX = H_FOX * HD_FOX
D_CONV = 512
CONV_WIDTH = 31
CONV_STATE = CONV_WIDTH - 1
H_MEM = 4
HD_MEM = 128
D_MEM = H_MEM * HD_MEM
N_MEM = 256
N_BRANCH = 3
PEER_HEADS = 8
PEER_NKEYS = 128
PEER_N = PEER_NKEYS * PEER_NKEYS
PEER_DK = 256
PEER_TOPK = 16
EPS = 1e-6

OFF_Q = 0
OFF_K = OFF_Q + D_FOX
OFF_V = OFF_K + D_FOX
OFF_F = OFF_V + D_FOX
OFF_GLU = OFF_F + H_FOX
OFF_MQ = OFF_GLU + 2 * D_CONV
OFF_G = OFF_MQ + D_MEM

LANES = 128
CONV_HALO = 32
VMEM_LIMIT = 56 * 1024 * 1024

F32 = jnp.float32
BF16 = jnp.bfloat16
NEG_INF = float("-inf")


def _dot(a, b):
    return jnp.dot(a.astype(BF16), b.astype(BF16), preferred_element_type=F32)


def _dot_nt(a, b):
    return lax.dot_general(a.astype(BF16), b.astype(BF16), (((1,), (1,)), ((), ())),
                           preferred_element_type=F32)


def _split3(x):
    hi = x.astype(BF16)
    r = x - hi.astype(F32)
    mid = r.astype(BF16)
    lo = (r - mid.astype(F32)).astype(BF16)
    return hi, mid, lo


def _dot_exact_rhs01(x, m01):
    hi, mid, lo = _split3(x)
    return (jnp.dot(hi, m01, preferred_element_type=F32) + jnp.dot(mid, m01, preferred_element_type=F32)
            + jnp.dot(lo, m01, preferred_element_type=F32))


def _rms_rows(x, w):
    return x * lax.rsqrt(jnp.mean(x * x, axis=-1, keepdims=True) + EPS) * w


def _params(sem, flags=None):
    return pltpu.CompilerParams(dimension_semantics=sem, vmem_limit_bytes=VMEM_LIMIT, flags=flags)


def _full(shape):
    nd = len(shape)
    return pl.BlockSpec(shape, lambda *_: (0,) * nd)


def _proj_kernel(x_ref, n1_ref, wqkv_ref, wf_ref, bf_ref, wglu_ref, wmq_ref, qn_ref, kn_ref, mqn_ref, bd_ref,
                 q_out, kf_out, kb_out, vf_out, vb_out, logf_out, logft_out, a_out, mq_out):
    x = x_ref[...]
    h = _rms_rows(x, n1_ref[...]).astype(BF16)
    z = jnp.dot(h, wqkv_ref[...], preferred_element_type=F32)
    bd = bd_ref[...]

    def head_norm(t, w):
        sq = t * t
        hi = sq.astype(BF16)
        lo = (sq - hi.astype(F32)).astype(BF16)
        ss = jnp.dot(hi, bd, preferred_element_type=F32) + jnp.dot(lo, bd, preferred_element_type=F32)
        return t * lax.rsqrt(ss * (1.0 / HD_FOX) + EPS) * w

    q = head_norm(z[:, OFF_Q:OFF_K], qn_ref[...])
    k = head_norm(z[:, OFF_K:OFF_V], kn_ref[...])
    v = z[:, OFF_V:OFF_F]
    q_out[...] = (q * (HD_FOX ** -0.5)).astype(BF16)
    kf_out[...] = k
    kb_out[...] = k.astype(BF16)
    vf_out[...] = v
    vb_out[...] = v.astype(BF16)

    zf = jnp.dot(h, wf_ref[...], preferred_element_type=F32) + bf_ref[...]
    lf = jnp.minimum(zf, 0.0) - jnp.log1p(jnp.exp(-jnp.abs(zf)))
    logf_out[...] = lf[:, :H_FOX]
    logft_out[...] = lf.T[:H_FOX, :]

    g = jnp.dot(h, wglu_ref[...], preferred_element_type=F32)
    a_out[...] = g[:, :D_CONV] * jax.nn.sigmoid(g[:, D_CONV:])

    mq = jnp.dot(h, wmq_ref[...], preferred_element_type=F32)
    mqn = mqn_ref[...]
    parts = []
    for hh in range(H_MEM):
        blk = mq[:, hh * HD_MEM:(hh + 1) * HD_MEM]
        parts.append(_rms_rows(blk, mqn) * (HD_MEM ** -0.5))
    mq_out[...] = jnp.concatenate(parts, axis=-1).astype(BF16)


def _project(x, w, tb):
    nt = x.shape[0]
    grid = (nt // tb,)
    row = lambda c: pl.BlockSpec((tb, c), lambda i: (i, 0))
    out_shape = (
        jax.ShapeDtypeStruct((nt, D_FOX), BF16),
        jax.ShapeDtypeStruct((nt, D_FOX), F32),
        jax.ShapeDtypeStruct((nt, D_FOX), BF16),
        jax.ShapeDtypeStruct((nt, D_FOX), F32),
        jax.ShapeDtypeStruct((nt, D_FOX), BF16),
        jax.ShapeDtypeStruct((nt, H_FOX), F32),
        jax.ShapeDtypeStruct((H_FOX, nt), F32),
        jax.ShapeDtypeStruct((nt, D_CONV), F32),
        jax.ShapeDtypeStruct((nt, D_MEM), BF16),
    )
    out_specs = (row(D_FOX), row(D_FOX), row(D_FOX), row(D_FOX), row(D_FOX), row(H_FOX),
                 pl.BlockSpec((H_FOX, tb), lambda i: (0, i)), row(D_CONV), row(D_MEM))
    ins = (x, w['norm1'], w['w_qkv'], w['w_f'], w['b_f'], w['w_glu'], w['w_mq'], w['fox_qn'], w['fox_kn'],
           w['mem_qn'], w['bd64'])
    in_specs = [row(D_MODEL)] + [_full(a.shape) for a in ins[1:]]
    return pl.pallas_call(_proj_kernel, grid=grid, in_specs=in_specs, out_specs=out_specs, out_shape=out_shape,
                          compiler_params=_params(("parallel",)), name="proj_in")(*ins)


def _cumsum_kernel(lt_ref, tri_ref, c_ref, carry_ref, *, blocks_per_seq):
    i = pl.program_id(0)

    @pl.when(i % blocks_per_seq == 0)
    def _():
        carry_ref[...] = jnp.zeros_like(carry_ref)

    c = _dot_exact_rhs01(lt_ref[...], tri_ref[...]) + carry_ref[:, 0:1]
    c_ref[0] = c
    carry_ref[...] = jnp.broadcast_to(c[:, -1:], carry_ref.shape)


def _cumsum_seq(logft, seq_len, tk, tri):
    nt = logft.shape[1]
    return pl.pallas_call(
        functools.partial(_cumsum_kernel, blocks_per_seq=seq_len // tk),
        grid=(nt // tk,),
        in_specs=[pl.BlockSpec((H_FOX, tk), lambda i: (0, i)), _full(tri.shape)],
        out_specs=pl.BlockSpec((1, H_FOX, tk), lambda i: (i, 0, 0)),
        out_shape=jax.ShapeDtypeStruct((nt // tk, H_FOX, tk), F32),
        scratch_shapes=[pltpu.VMEM((H_FOX, LANES), F32)],
        compiler_params=_params(("arbitrary",)), name="logf_cumsum")(logft, tri)


def _fox_prompt_kernel(q_ref, k_ref, v_ref, c_ref, o_ref, *, tq):
    hp = pl.program_id(1)
    qi = pl.program_id(2)
    lane = lax.broadcasted_iota(jnp.int32, (tq, LANES), 1)
    q2 = q_ref[...]
    zero = jnp.zeros_like(q2)
    qs = (jnp.where(lane < HD_FOX, q2, zero), jnp.where(lane >= HD_FOX, q2, zero))

    def attend(ki, carry, masked):
        k0 = pl.multiple_of(ki * tq, tq)
        k2 = k_ref[pl.ds(k0, tq), :]
        v2 = v_ref[pl.ds(k0, tq), :]
        out = []
        for j in range(2):
            m, l, acc = carry[j]
            s = _dot_nt(qs[j], k2) - c_ref[ki, pl.ds(2 * hp + j, 1), :]
            if masked:
                r = lax.broadcasted_iota(jnp.int32, (tq, tq), 0)
                c = lax.broadcasted_iota(jnp.int32, (tq, tq), 1)
                s = jnp.where(c <= r, s, NEG_INF)
            m_new = jnp.maximum(m, jnp.max(s, axis=-1, keepdims=True))
            alpha = jnp.exp(m - m_new)
            p = jnp.exp(s - m_new)
            l = alpha * l + jnp.sum(p, axis=-1, keepdims=True)
            acc = alpha * acc + jnp.dot(p.astype(BF16), v2, preferred_element_type=F32)
            out.append((m_new, l, acc))
        return tuple(out)

    init = tuple((jnp.full((tq, 1), NEG_INF, F32), jnp.zeros((tq, 1), F32), jnp.zeros((tq, LANES), F32))
                 for _ in range(2))
    carry = lax.fori_loop(0, qi, lambda ki, c: attend(ki, c, False), init)
    (_, l0, a0), (_, l1, a1) = attend(qi, carry, True)
    o_ref[...] = jnp.where(lane < HD_FOX, a0 / l0, a1 / l1).astype(o_ref.dtype)


def _fox_prompt(qb, kb, vb, cum, n_seq, seq_len, tq):
    nq = seq_len // tq
    q3 = qb.reshape(n_seq, seq_len, D_FOX)
    k3 = kb.reshape(n_seq, seq_len, D_FOX)
    v3 = vb.reshape(n_seq, seq_len, D_FOX)
    out = pl.pallas_call(
        functools.partial(_fox_prompt_kernel, tq=tq),
        grid=(n_seq, H_FOX // 2, nq),
        in_specs=[pl.BlockSpec((None, tq, LANES), lambda n, h, i: (n, i, h)),
                  pl.BlockSpec((None, seq_len, LANES), lambda n, h, i: (n, 0, h)),
                  pl.BlockSpec((None, seq_len, LANES), lambda n, h, i: (n, 0, h)),
                  pl.BlockSpec((nq, H_FOX, tq), lambda n, h, i: (n, 0, 0))],
        out_specs=pl.BlockSpec((None, tq, LANES), lambda n, h, i: (n, i, h)),
        out_shape=jax.ShapeDtypeStruct((n_seq, seq_len, D_FOX), BF16),
        compiler_params=_params(("parallel", "parallel", "arbitrary")), name="fox_prompt")(q3, k3, v3, cum)
    return out.reshape(n_seq * seq_len, D_FOX)


def _fox_sample_kernel(q_ref, kn_ref, vn_ref, ln_ref, ck_ref, cv_ref, cl_ref, tri_ref, o_ref,
                       qbd_ref, m_ref, l_ref, acc_ref, carry_ref, *, n_cache_blocks, s_new):
    ki = pl.program_id(1)
    rows = H_FOX * s_new

    @pl.when(ki == 0)
    def _():
        q = q_ref[...]
        lane = lax.broadcasted_iota(jnp.int32, q.shape, 1)
        zero = jnp.zeros_like(q)
        for h in range(H_FOX):
            sel = (lane >= h * HD_FOX) & (lane < (h + 1) * HD_FOX)
            qbd_ref[h * s_new:(h + 1) * s_new, :] = jnp.where(sel, q, zero)
        m_ref[...] = jnp.full_like(m_ref, NEG_INF)
        l_ref[...] = jnp.zeros_like(l_ref)
        acc_ref[...] = jnp.zeros_like(acc_ref)
        carry_ref[...] = jnp.zeros_like(carry_ref)

    def update(s, vblk):
        m = m_ref[...]
        m_new = jnp.maximum(m, jnp.max(s, axis=-1, keepdims=True))
        alpha = jnp.exp(m - m_new)
        p = jnp.exp(s - m_new)
        l_ref[...] = alpha * l_ref[...] + jnp.sum(p, axis=-1, keepdims=True)
        acc_ref[...] = alpha * acc_ref[...] + jnp.dot(p.astype(BF16), vblk, preferred_element_type=F32)
        m_ref[...] = m_new

    def biased(s, cum):
        return jnp.concatenate([s[h * s_new:(h + 1) * s_new, :] - cum[h:h + 1, :] for h in range(H_FOX)], axis=0)

    @pl.when(ki < n_cache_blocks)
    def _():
        tk = ck_ref.shape[0]
        cum = _dot_exact_rhs01(cl_ref[...], tri_ref[...]) + carry_ref[:, 0:1]
        carry_ref[...] = jnp.broadcast_to(cum[:, -1:], carry_ref.shape)
        s = _dot_nt(qbd_ref[...], ck_ref[...])
        update(biased(s, cum), cv_ref[...].astype(BF16))
        del tk

    @pl.when(ki == n_cache_blocks)
    def _():
        tri = tri_ref[0:s_new, 0:s_new]
        cum = _dot_exact_rhs01(ln_ref[...], tri) + carry_ref[:, 0:1]
        s = biased(_dot_nt(qbd_ref[...], kn_ref[...]), cum)
        t = lax.broadcasted_iota(jnp.int32, (rows, s_new), 0) % s_new
        c = lax.broadcasted_iota(jnp.int32, (rows, s_new), 1)
        update(jnp.where(c <= t, s, NEG_INF), vn_ref[...])
        o = acc_ref[...] / l_ref[...]
        lane = lax.broadcasted_iota(jnp.int32, (s_new, D_FOX), 1)
        res = jnp.zeros((s_new, D_FOX), F32)
        for h in range(H_FOX):
            sel = (lane >= h * HD_FOX) & (lane < (h + 1) * HD_FOX)
            res = jnp.where(sel, o[h * s_new:(h + 1) * s_new, :], res)
        o_ref[...] = res.astype(o_ref.dtype)


def _fox_sample(qb, kb, vb, logft, ck, cv, clogf, tri, tk):
    n_seq, past = ck.shape[0], ck.shape[1]
    s_new = qb.shape[0] // n_seq
    ncb = past // tk
    q3 = qb.reshape(n_seq, s_new, D_FOX)
    k3 = kb.reshape(n_seq, s_new, D_FOX)
    v3 = vb.reshape(n_seq, s_new, D_FOX)
    ln = jnp.swapaxes(logft.reshape(H_FOX, n_seq, s_new), 0, 1)
    cl = jnp.swapaxes(clogf.astype(F32), 1, 2)
    rows = H_FOX * s_new
    new = pl.BlockSpec((None, s_new, D_FOX), lambda n, k: (n, 0, 0))
    cache = pl.BlockSpec((None, tk, D_FOX), lambda n, k: (n, jnp.minimum(k, ncb - 1), 0))
    out = pl.pallas_call(
        functools.partial(_fox_sample_kernel, n_cache_blocks=ncb, s_new=s_new),
        grid=(n_seq, ncb + 1),
        in_specs=[new, new, new,
                  pl.BlockSpec((None, H_FOX, s_new), lambda n, k: (n, 0, 0)),
                  cache, cache,
                  pl.BlockSpec((None, H_FOX, tk), lambda n, k: (n, 0, jnp.minimum(k, ncb - 1))),
                  _full(tri.shape)],
        out_specs=new,
        out_shape=jax.ShapeDtypeStruct((n_seq, s_new, D_FOX), BF16),
        scratch_shapes=[pltpu.VMEM((rows, D_FOX), BF16), pltpu.VMEM((rows, 1), F32), pltpu.VMEM((rows, 1), F32),
                        pltpu.VMEM((rows, D_FOX), F32), pltpu.VMEM((H_FOX, LANES), F32)],
        compiler_params=_params(("parallel", "arbitrary")), name="fox_sample")(q3, k3, v3, ln, ck, cv, cl, tri)
    return out.reshape(n_seq * s_new, D_FOX)


def _conv_kernel(a_ref, tail_ref, hist_ref, wdw_ref, bdw_ref, lnw_ref, lnb_ref, y_ref, win_ref, *, tb):
    i = pl.program_id(1)

    @pl.when(i == 0)
    def _():
        win_ref[0:CONV_HALO, :] = hist_ref[...]

    @pl.when(i > 0)
    def _():
        win_ref[0:CONV_HALO, :] = tail_ref[...]

    win_ref[CONV_HALO:CONV_HALO + tb, :] = a_ref[...]
    wdw = wdw_ref[...]
    first = CONV_HALO - CONV_STATE
    y = jnp.zeros((tb, D_CONV), F32) + bdw_ref[...]
    for w in range(CONV_WIDTH):
        y = y + win_ref[first + w:first + w + tb, :] * wdw[w:w + 1, :]
    mu = jnp.mean(y, axis=-1, keepdims=True)
    yc = y - mu
    var = jnp.mean(yc * yc, axis=-1, keepdims=True)
    yn = yc * lax.rsqrt(var + EPS) * lnw_ref[...] + lnb_ref[...]
    y_ref[...] = (yn * jax.nn.sigmoid(yn)).astype(y_ref.dtype)


def _conv_branch(a, hist, w, n_seq, seq_len, tb):
    a3 = a.reshape(n_seq, seq_len, D_CONV)
    hist32 = jnp.pad(hist.astype(F32), ((0, 0), (CONV_HALO - CONV_STATE, 0), (0, 0)))
    r = tb // CONV_HALO
    out = pl.pallas_call(
        functools.partial(_conv_kernel, tb=tb),
        grid=(n_seq, seq_len // tb),
        in_specs=[pl.BlockSpec((None, tb, D_CONV), lambda n, i: (n, i, 0)),
                  pl.BlockSpec((None, CONV_HALO, D_CONV), lambda n, i: (n, jnp.maximum(i * r - 1, 0), 0)),
                  pl.BlockSpec((None, CONV_HALO, D_CONV), lambda n, i: (n, 0, 0)),
                  _full(w['w_dw'].shape), _full(w['b_dw'].shape), _full(w['ln_w'].shape), _full(w['ln_b'].shape)],
        out_specs=pl.BlockSpec((None, tb, D_CONV), lambda n, i: (n, i, 0)),
        out_shape=jax.ShapeDtypeStruct((n_seq, seq_len, D_CONV), BF16),
        scratch_shapes=[pltpu.VMEM((CONV_HALO + tb, D_CONV), F32)],
        compiler_params=_params(("parallel", "arbitrary")), name="conv_branch")(
            a3, a3, hist32, w['w_dw'], w['b_dw'], w['ln_w'], w['ln_b'])
    return out.reshape(n_seq * seq_len, D_CONV)


def _memkv_kernel(mem_ref, nw_ref, wkv_ref, kn_ref, mk_ref, mv_ref):
    h = _rms_rows(mem_ref[...], nw_ref[...]).astype(BF16)
    m = jnp.dot(h, wkv_ref[...], preferred_element_type=F32)
    kn = kn_ref[...]
    parts = [_rms_rows(m[:, hh * HD_MEM:(hh + 1) * HD_MEM], kn) for hh in range(H_MEM)]
    mk_ref[...] = jnp.concatenate(parts, axis=-1)
    mv_ref[...] = m[:, D_MEM:]


def _mem_kv(mem, w):
    nt = mem.shape[0]
    tb = N_MEM
    row = lambda c: pl.BlockSpec((tb, c), lambda i: (i, 0))
    return pl.pallas_call(
        _memkv_kernel, grid=(nt // tb,),
        in_specs=[row(D_MODEL), _full(w['mem_norm'].shape), _full(w['w_mem_kv'].shape), _full(w['mem_kn'].shape)],
        out_specs=(row(D_MEM), row(D_MEM)),
        out_shape=(jax.ShapeDtypeStruct((nt, D_MEM), F32), jax.ShapeDtypeStruct((nt, D_MEM), F32)),
        compiler_params=_params(("parallel",)), name="mem_kv")(mem, w['mem_norm'], w['w_mem_kv'], w['mem_kn'])


def _merge_kernel(x_ref, of_ref, yc_ref, mq_ref, mk_ref, mv_ref, n1_ref, wg_ref, wfo_ref, wco_ref, wmo_ref,
                  wout_ref, x1_ref):
    x = x_ref[...]
    h = _rms_rows(x, n1_ref[...]).astype(BF16)
    mq = mq_ref[...]
    mk = mk_ref[...].astype(BF16)
    mv = mv_ref[...].astype(BF16)
    heads = []
    for hh in range(H_MEM):
        sl = slice(hh * HD_MEM, (hh + 1) * HD_MEM)
        s = _dot_nt(mq[:, sl], mk[:, sl])
        p = jnp.exp(s - jnp.max(s, axis=-1, keepdims=True))
        p = p / jnp.sum(p, axis=-1, keepdims=True)
        heads.append(jnp.dot(p.astype(BF16), mv[:, sl], preferred_element_type=F32))
    o_mem = jnp.concatenate(heads, axis=-1).astype(BF16)

    def gate(b):
        zg = jnp.dot(h, wg_ref[:, b * D_MODEL:(b + 1) * D_MODEL], preferred_element_type=F32)
        return jax.nn.sigmoid(zg)

    merged = gate(0) * jnp.dot(of_ref[...], wfo_ref[...], preferred_element_type=F32)
    merged = merged + gate(1) * jnp.dot(yc_ref[...], wco_ref[...], preferred_element_type=F32)
    merged = merged + gate(2) * jnp.dot(o_mem, wmo_ref[...], preferred_element_type=F32)
    x1_ref[...] = x + jnp.dot(merged.astype(BF16), wout_ref[...], preferred_element_type=F32)


def _merge(x, o_fox, y_conv, mqn, mk, mv, w, seq_len, tb):
    nt = x.shape[0]
    bps = seq_len // tb
    row = lambda c: pl.BlockSpec((tb, c), lambda i: (i, 0))
    mem = pl.BlockSpec((N_MEM, D_MEM), lambda i: (i // bps, 0))
    ws = (w['norm1'], w['w_g'], w['w_fox_o'], w['w_conv_o'], w['w_mem_o'], w['w_out'])
    return pl.pallas_call(
        _merge_kernel, grid=(nt // tb,),
        in_specs=[row(D_MODEL), row(D_FOX), row(D_CONV), row(D_MEM), mem, mem] + [_full(a.shape) for a in ws],
        out_specs=row(D_MODEL),
        out_shape=jax.ShapeDtypeStruct((nt, D_MODEL), F32),
        compiler_params=_params(("parallel",)), name="merge")(x, o_fox, y_conv, mqn, mk, mv, *ws)


def _top_sorted(vals, n, out_ref, with_rank=False):
    def body(r, carry):
        v = carry[0]
        m = jnp.max(v, axis=0, keepdims=True)
        out_ref[pl.ds(r, 1), :] = m
        hit = v == m
        nxt = (jnp.where(hit, NEG_INF, v),)
        if with_rank:
            nxt += (jnp.where(hit, r.astype(F32), carry[1]),)
        return nxt

    init = (vals,) + ((jnp.full(vals.shape, float(n), F32),) if with_rank else ())
    out = lax.fori_loop(0, n, body, init)
    return out[1] if with_rank else None


def _peer_route_kernel(x_ref, n2_ref, pqt_ref, k1_ref, k2_ref, xnt_ref, a1_ref, cnt1_ref, e2_ref, rank2_ref,
                       qt_ref, v1_ref, v2_ref, tt_ref):
    xn32 = _rms_rows(x_ref[...], n2_ref[...])
    xn = xn32.astype(BF16)
    xnt_ref[...] = xn32.T.astype(BF16)
    qt_ref[...] = _dot_nt(pqt_ref[...], xn).astype(BF16)
    half = PEER_DK // 2
    k = PEER_TOPK
    k2 = PEER_TOPK // 2

    def per_head(h, _):
        q1 = qt_ref[pl.ds(pl.multiple_of(h * PEER_DK, PEER_DK), half), :]
        q2 = qt_ref[pl.ds(pl.multiple_of(h * PEER_DK + half, half), half), :]
        s1 = jnp.dot(k1_ref[h], q1, preferred_element_type=F32)
        s2 = jnp.dot(k2_ref[h], q2, preferred_element_type=F32)
        _top_sorted(s1, k, v1_ref)
        rank2 = _top_sorted(s2, k, v2_ref, with_rank=True)
        v1 = v1_ref[...]
        v2 = v2_ref[...]
        pieces = [v1 + v2[0:1, :]]
        pieces += [v1[0:k2, :] + v2[b:b + 1, :] for b in range(1, k2)]
        pieces += [v1[0:1, :] + v2[k2:k, :]]
        _top_sorted(jnp.concatenate(pieces, axis=0), k, tt_ref)
        tt = tt_ref[...]
        tau = tt[k - 1:k, :]
        z = jnp.sum(jnp.exp(tt - tt[0:1, :]), axis=0, keepdims=True)
        cnt = jnp.zeros_like(s1)
        for b in range(k2):
            cnt = cnt + jnp.where(s1 + v2[b:b + 1, :] >= tau, 1.0, 0.0)
        tail = jnp.sum(jnp.where(v1[0:1, :] + v2[k2:k, :] >= tau, 1.0, 0.0), axis=0, keepdims=True)
        cnt = cnt + jnp.where(s1 == v1[0:1, :], tail, 0.0)
        a1_ref[h] = jnp.exp(s1 - v1[0:1, :]) / z
        cnt1_ref[h] = cnt
        e2_ref[h] = jnp.exp(s2 - v2[0:1, :]).astype(BF16)
        rank2_ref[h] = rank2.astype(BF16)
        return 0

    lax.fori_loop(0, PEER_HEADS, per_head, 0)


def _peer_route(x1, w, tb):
    nt = x1.shape[0]
    sel = pl.BlockSpec((PEER_HEADS, PEER_NKEYS, tb), lambda i: (0, 0, i))
    sel32 = jax.ShapeDtypeStruct((PEER_HEADS, PEER_NKEYS, nt), F32)
    sel16 = jax.ShapeDtypeStruct((PEER_HEADS, PEER_NKEYS, nt), BF16)
    return pl.pallas_call(
        _peer_route_kernel, grid=(nt // tb,),
        in_specs=[pl.BlockSpec((tb, D_MODEL), lambda i: (i, 0)), _full(w['norm2'].shape), _full(w['pq_t'].shape),
                  _full(w['pk1'].shape), _full(w['pk2'].shape)],
        out_specs=(pl.BlockSpec((D_MODEL, tb), lambda i: (0, i)), sel, sel, sel, sel),
        out_shape=(jax.ShapeDtypeStruct((D_MODEL, nt), BF16), sel32, sel32, sel16, sel16),
        scratch_shapes=[pltpu.VMEM((PEER_HEADS * PEER_DK, tb), BF16), pltpu.VMEM((PEER_TOPK, tb), F32),
                        pltpu.VMEM((PEER_TOPK, tb), F32), pltpu.VMEM((PEER_TOPK, tb), F32)],
        compiler_params=_params(("parallel",)), name="peer_route")(x1, w['norm2'], w['pq_t'], w['pk1'], w['pk2'])


EXPERT_CHUNK = 512
KEY_STRIP = 32


def _peer_dense_kernel(x1_ref, xnt_ref, a1_ref, cnt1_ref, e2_ref, rank2_ref, u_ref, vt_ref, y_ref,
                       w0_ref, w1_ref, acc_ref, *, chunks, n_tiles):
    e = pl.program_id(1)
    tb = xnt_ref.shape[1]
    groups = EXPERT_CHUNK // PEER_NKEYS
    acc_rows = D_MODEL // groups
    w_refs = (w0_ref, w1_ref)
    zero = jnp.zeros((KEY_STRIP, tb), BF16)

    def weigh(c, g, w_w):
        grp = c * groups + g
        i1 = e * (chunks * groups) + grp
        r0 = pl.multiple_of(grp * PEER_NKEYS, PEER_NKEYS)
        st = jnp.dot(u_ref[pl.ds(r0, PEER_NKEYS), :], xnt_ref[...], preferred_element_type=F32)
        rows = lambda ref, h: jnp.broadcast_to(ref[h, pl.ds(i1, 1), :], (KEY_STRIP, tb)).astype(BF16)
        cnt_rows = [rows(cnt1_ref, h) for h in range(PEER_HEADS)]
        a1_rows = [rows(a1_ref, h) for h in range(PEER_HEADS)]
        for k0 in range(0, PEER_NKEYS, KEY_STRIP):
            ks = slice(k0, k0 + KEY_STRIP)
            gate = zero
            for h in range(PEER_HEADS):
                keep = rank2_ref[h, ks, :] < cnt_rows[h]
                gate = gate + jnp.where(keep, e2_ref[h, ks, :], zero) * a1_rows[h]
            s = st[ks, :]
            act = 0.5 * s * (1.0 + lax.erf(s * (1.0 / math.sqrt(2.0))))
            w_w[pl.ds(pl.multiple_of(r0 + k0, KEY_STRIP), KEY_STRIP), :] = gate * act.astype(BF16)

    def accumulate(c, g, w_r):
        rows = slice(g * acc_rows, (g + 1) * acc_rows)
        k0 = pl.multiple_of(c * EXPERT_CHUNK, EXPERT_CHUNK)
        acc_ref[rows, :] += jnp.dot(vt_ref[c, rows, :], w_r[pl.ds(k0, EXPERT_CHUNK), :],
                                    preferred_element_type=F32)

    def stage(parity, do_weigh, do_acc):
        def body(c, _):
            for g in range(groups):
                if do_weigh:
                    weigh(c, g, w_refs[parity])
                if do_acc:
                    accumulate(c, g, w_refs[1 - parity])
            return 0
        lax.fori_loop(0, chunks, body, 0)

    @pl.when(e == 0)
    def _():
        acc_ref[...] = jnp.zeros_like(acc_ref)
        stage(0, True, False)

    for parity in range(2):
        @pl.when((e >= 1) & (e < n_tiles) & (e % 2 == parity))
        def _():
            stage(parity, True, True)

    @pl.when(e == n_tiles)
    def _():
        stage(n_tiles % 2, False, True)
        y_ref[...] = x1_ref[...] + acc_ref[...].T


def _peer_dense(x1, xnt, a1, cnt1, e2, rank2, w, tb, eb):
    nt = x1.shape[0]
    chunks = eb // EXPERT_CHUNK
    n_tiles = PEER_N // eb
    once = pl.Buffered(1)
    sel = pl.BlockSpec((PEER_HEADS, PEER_NKEYS, tb), lambda i, e: (0, 0, i), pipeline_mode=once)
    return pl.pallas_call(
        functools.partial(_peer_dense_kernel, chunks=chunks, n_tiles=n_tiles),
        grid=(nt // tb, n_tiles + 1),
        in_specs=[pl.BlockSpec((tb, D_MODEL), lambda i, e: (i, 0), pipeline_mode=once),
                  pl.BlockSpec((D_MODEL, tb), lambda i, e: (0, i), pipeline_mode=once),
                  sel, sel, sel, sel,
                  pl.BlockSpec((eb, D_MODEL), lambda i, e: (jnp.minimum(e, n_tiles - 1), 0)),
                  pl.BlockSpec((chunks, D_MODEL, EXPERT_CHUNK), lambda i, e: (jnp.maximum(e - 1, 0), 0, 0))],
        out_specs=pl.BlockSpec((tb, D_MODEL), lambda i, e: (i, 0)),
        out_shape=jax.ShapeDtypeStruct((nt, D_MODEL), F32),
        scratch_shapes=[pltpu.VMEM((eb, tb), BF16), pltpu.VMEM((eb, tb), BF16), pltpu.VMEM((D_MODEL, tb), F32)],
        compiler_params=_params(("parallel", "arbitrary")), name="peer_dense")(
            x1, xnt, a1, cnt1, e2, rank2, w['pu'], w['pv_t'])


def _block_sizes(n_seq, seq_len):
    nt = n_seq * seq_len
    return dict(
        proj=min(512, nt),
        att=min(512, seq_len),
        conv=min(512, seq_len),
        merge=min(256, seq_len),
        route=min(256, nt),
        dense=min(512, nt),
        experts=2048,
    )


def _layer(x, hist, mk, mv, w, n_seq, seq_len, cache):
    bs = _block_sizes(n_seq, seq_len)
    qb, kf, kb, vf, vb, logf, logft, a, mqn = _project(x, w, bs['proj'])
    if cache is None:
        cum = _cumsum_seq(logft, seq_len, bs['att'], w['tri'][:bs['att'], :bs['att']])
        o_fox = _fox_prompt(qb, kb, vb, cum, n_seq, seq_len, bs['att'])
    else:
        ck, cv, clogf = cache
        o_fox = _fox_sample(qb, kb, vb, logft, ck, cv, clogf, w['tri'], w['tri'].shape[0])
    y_conv = _conv_branch(a, hist, w, n_seq, seq_len, bs['conv'])
    x1 = _merge(x, o_fox, y_conv, mqn, mk, mv, w, seq_len, bs['merge'])
    xnt, a1, cnt1, e2, rank2 = _peer_route(x1, w, bs['route'])
    y = _peer_dense(x1, xnt, a1, cnt1, e2, rank2, w, bs['dense'], bs['experts'])
    conv_state = a.reshape(n_seq, seq_len, D_CONV)[:, seq_len - CONV_STATE:, :]
    return y, kf, vf, logf, conv_state


def kernel(x_prompt, x_sample, mem_prompt, cache_fox_k, cache_fox_v, cache_fox_logf, cache_conv, cache_mem_k, cache_mem_v, norm1_w, w_in, b_forget, fox_q_norm, fox_k_norm, w_fox_o, w_dw, b_dw, conv_ln_w, conv_ln_b, w_conv_o, mem_norm_w, w_mem_kv, mem_q_norm, mem_k_norm, w_mem_o, w_out, norm2_w, peer_w_q, peer_keys1, peer_keys2, peer_u, peer_v):
    depth = w_in.shape[0]
    batch, seq = x_prompt.shape[:2]
    dbatch, dseq = x_sample.shape[:2]
    past = cache_fox_k.shape[2]
    tri_n = 1024
    tri = (lax.broadcasted_iota(jnp.int32, (tri_n, tri_n), 0) <= lax.broadcasted_iota(jnp.int32, (tri_n, tri_n), 1)
           ).astype(BF16)
    gid = lax.broadcasted_iota(jnp.int32, (D_FOX, D_FOX), 0) // HD_FOX
    bd64 = (gid == gid.T).astype(BF16)

    xp = x_prompt.reshape(batch * seq, D_MODEL)
    xs = x_sample.reshape(dbatch * dseq, D_MODEL)
    outs = [[] for _ in range(10)]
    for l in range(depth):
        wi = w_in[l]
        row2 = lambda v: v.reshape(1, -1).astype(F32)
        w = dict(
            norm1=row2(norm1_w[l]),
            w_qkv=wi[:, OFF_Q:OFF_F].astype(BF16),
            w_f=jnp.pad(wi[:, OFF_F:OFF_GLU], ((0, 0), (0, LANES - H_FOX))).astype(BF16),
            b_f=jnp.pad(b_forget[l].astype(F32), (0, LANES - H_FOX)).reshape(1, LANES),
            w_glu=wi[:, OFF_GLU:OFF_MQ].astype(BF16),
            w_mq=wi[:, OFF_MQ:OFF_G].astype(BF16),
            w_g=wi[:, OFF_G:].astype(BF16),
            fox_qn=row2(jnp.tile(fox_q_norm[l], H_FOX)),
            fox_kn=row2(jnp.tile(fox_k_norm[l], H_FOX)),
            mem_qn=row2(mem_q_norm[l]),
            mem_kn=row2(mem_k_norm[l]),
            mem_norm=row2(mem_norm_w[l]),
            w_mem_kv=w_mem_kv[l].astype(BF16),
            bd64=bd64, tri=tri,
            w_dw=w_dw[l].astype(F32), b_dw=row2(b_dw[l]), ln_w=row2(conv_ln_w[l]), ln_b=row2(conv_ln_b[l]),
            w_fox_o=w_fox_o[l].astype(BF16), w_conv_o=w_conv_o[l].astype(BF16), w_mem_o=w_mem_o[l].astype(BF16),
            w_out=w_out[l].astype(BF16),
            norm2=row2(norm2_w[l]),
            pq_t=peer_w_q[l].T.astype(BF16),
            pk1=peer_keys1[l].astype(BF16), pk2=peer_keys2[l].astype(BF16),
            pu=peer_u[l].astype(BF16), pv_t=jnp.swapaxes(peer_v[l].astype(BF16).reshape(PEER_N // EXPERT_CHUNK, EXPERT_CHUNK, D_MODEL), 1, 2),
        )
        mk_p, mv_p = _mem_kv(mem_prompt.reshape(batch * N_MEM, D_MODEL), w)
        hist0 = jnp.zeros((batch, CONV_STATE, D_CONV), F32)
        xp, k_p, v_p, f_p, c_p = _layer(xp, hist0, mk_p, mv_p, w, batch, seq, None)
        cache = (cache_fox_k[l].reshape(dbatch, past, D_FOX), cache_fox_v[l].reshape(dbatch, past, D_FOX),
                 cache_fox_logf[l])
        mk_s = cache_mem_k[l].reshape(dbatch * N_MEM, D_MEM).astype(F32)
        mv_s = cache_mem_v[l].reshape(dbatch * N_MEM, D_MEM).astype(F32)
        xs, k_s, v_s, f_s, c_s = _layer(xs, cache_conv[l], mk_s, mv_s, w, dbatch, dseq, cache)
        vals = (k_p.reshape(batch, seq, H_FOX, HD_FOX), v_p.reshape(batch, seq, H_FOX, HD_FOX),
                f_p.reshape(batch, seq, H_FOX), c_p,
                mk_p.reshape(batch, N_MEM, H_MEM, HD_MEM), mv_p.reshape(batch, N_MEM, H_MEM, HD_MEM),
                k_s.reshape(dbatch, dseq, H_FOX, HD_FOX), v_s.reshape(dbatch, dseq, H_FOX, HD_FOX),
                f_s.reshape(dbatch, dseq, H_FOX), c_s)
        for o, v in zip(outs, vals):
            o.append(v)
    return (xp.reshape(batch, seq, D_MODEL), xs.reshape(dbatch, dseq, D_MODEL)) + tuple(jnp.stack(o) for o in outs)
```
